```python
import math
import jax, jax.numpy as jnp
from jax import lax
import numpy as np

D_MODEL = 2048
BATCH = 8
SEQ = 2048
DEPTH = 1

N_META = 16
GRID_W = 64
Q_BLOCK = 128
HEAD_DIM = 128
MIX_WIDTH = D_MODEL
A_WIDTH = MIX_WIDTH // 2
A_HEADS = A_WIDTH // HEAD_DIM
A_KV_HEADS = 2
A_GROUP = A_HEADS // A_KV_HEADS
B_WIDTH = MIX_WIDTH - A_WIDTH
B_V_DIM = HEAD_DIM
B_QK_DIM = HEAD_DIM // 2
B_HEADS = B_WIDTH // B_V_DIM

ROPE_THETA = 10000.0
NORM_EPS = 1e-6

A_Q_COLS = A_HEADS * HEAD_DIM
A_KV_COLS = A_KV_HEADS * HEAD_DIM
A_GATE_COLS = A_WIDTH
B_QK_COLS = B_HEADS * 2 * B_QK_DIM
B_V_COLS = B_HEADS * B_V_DIM
B_GATE_COLS = B_WIDTH
IN_PROJ_SIZES = (A_Q_COLS, A_KV_COLS, A_KV_COLS, A_GATE_COLS,
                 B_QK_COLS, B_QK_COLS, B_V_COLS, B_GATE_COLS)
IN_PROJ_WIDTH = sum(IN_PROJ_SIZES)
SPLIT_POINTS = tuple(int(c) for c in np.cumsum(IN_PROJ_SIZES)[:-1])

kernel_name = "hybrid_gqa_axial_diffattn_alibi_sandwich"


def rmsnorm(x, g):
    xf = x.astype(jnp.float32)
    r = lax.rsqrt(jnp.mean(xf * xf, axis=-1, keepdims=True) + NORM_EPS)
    return (xf * r * g.astype(jnp.float32)).astype(x.dtype)


def alibi_slopes(n_heads):
    return 2.0 ** (-8.0 * (jnp.arange(n_heads, dtype=jnp.float32) + 1.0) / n_heads)


def axial_angles(grid_pos):
    axis_dim = HEAD_DIM // 2
    inv_freq = ROPE_THETA ** (-jnp.arange(0, axis_dim, 2, dtype=jnp.float32) / axis_dim)
    return grid_pos.astype(jnp.float32)[:, None] * inv_freq[None, :]


def rope_half(x, ang):
    x1, x2 = jnp.split(x, 2, axis=-1)
    c, s = jnp.cos(ang), jnp.sin(ang)
    return jnp.concatenate([x1 * c - x2 * s, x2 * c + x1 * s], axis=-1)


def axial_rope(x, ang_row, ang_col):
    xf = x.astype(jnp.float32)
    xr, xc = jnp.split(xf, 2, axis=-1)
    return jnp.concatenate([rope_half(xr, ang_row), rope_half(xc, ang_col)], axis=-1).astype(x.dtype)


def sweep_query_blocks(attend, qs, qpos):
    meta_out = attend(tuple(q[..., :N_META, :] for q in qs), qpos[:N_META])
    n_real = qpos.shape[0] - N_META
    n_blk = n_real // Q_BLOCK

    def to_blocks(a):
        a = a[..., N_META:, :]
        a = a.reshape(a.shape[:-2] + (n_blk, Q_BLOCK, a.shape[-1]))
        return jnp.moveaxis(a, -3, 0)

    blk_q = tuple(to_blocks(q) for q in qs)
    blk_pos = qpos[N_META:].reshape(n_blk, Q_BLOCK)
    out = lax.map(lambda args: attend(args[0], args[1]), (blk_q, blk_pos))
    out = jnp.moveaxis(out, 0, -3)
    out = out.reshape(out.shape[:-3] + (n_real, out.shape[-1]))
    return jnp.concatenate([meta_out, out], axis=-2)


def heads_first(t, n_heads, d):
    b, l, _ = t.shape
    return t.reshape(b, l, n_heads, d).transpose(0, 2, 1, 3)


def gqa_axial_mixer(qa, ka, va, q_g, k_g, ang_row, ang_col, pos):
    b, l, _ = qa.shape
    q = axial_rope(rmsnorm(heads_first(qa, A_HEADS, HEAD_DIM), q_g), ang_row, ang_col)
    k = axial_rope(rmsnorm(heads_first(ka, A_KV_HEADS, HEAD_DIM), k_g), ang_row, ang_col)
    v = heads_first(va, A_KV_HEADS, HEAD_DIM)
    q = q.reshape(b, A_KV_HEADS, A_GROUP, l, HEAD_DIM)
    scale = 1.0 / math.sqrt(HEAD_DIM)

    def attend(qs, qpos):
        s = jnp.einsum('bkgqd,bksd->bkgqs', qs[0], k).astype(jnp.float32) * scale
        p = jax.nn.softmax(s, axis=-1)
        return jnp.einsum('bkgqs,bksd->bkgqd', p.astype(v.dtype), v)

    o = sweep_query_blocks(attend, (q,), pos)
    return o.transpose(0, 3, 1, 2, 4).reshape(b, l, A_WIDTH)


def diff_attn_mixer(qb, kb, vb, lq1, lk1, lq2, lk2, sub_g, lambda_init, pos):
    b, l, _ = qb.shape
    q = qb.reshape(b, l, B_HEADS, 2, B_QK_DIM).transpose(0, 2, 3, 1, 4)
    k = kb.reshape(b, l, B_HEADS, 2, B_QK_DIM).transpose(0, 2, 3, 1, 4)
    q1, q2 = q[:, :, 0], q[:, :, 1]
    k1, k2 = k[:, :, 0], k[:, :, 1]
    v = heads_first(vb, B_HEADS, B_V_DIM)
    lam = (jnp.exp(jnp.sum(lq1.astype(jnp.float32) * lk1.astype(jnp.float32)))
           - jnp.exp(jnp.sum(lq2.astype(jnp.float32) * lk2.astype(jnp.float32)))
           + lambda_init)
    slopes = alibi_slopes(B_HEADS)
    kpos = pos.astype(jnp.float32)
    scale = 1.0 / math.sqrt(B_QK_DIM)

    def attend(qs, qpos):
        dist = jnp.abs(qpos.astype(jnp.float32)[:, None] - kpos[None, :])
        bias = -slopes[:, None, None] * dist[None]
        s1 = jnp.einsum('bhqd,bhkd->bhqk', qs[0], k1).astype(jnp.float32) * scale + bias
        s2 = jnp.einsum('bhqd,bhkd->bhqk', qs[1], k2).astype(jnp.float32) * scale + bias
        a = jax.nn.softmax(s1, axis=-1) - lam * jax.nn.softmax(s2, axis=-1)
        return jnp.einsum('bhqk,bhkd->bhqd', a.astype(v.dtype), v)

    o = sweep_query_blocks(attend, (q1, q2), pos)
    o = rmsnorm(o, sub_g) * (1.0 - lambda_init)
    return o.transpose(0, 2, 1, 3).reshape(b, l, B_WIDTH)


def setup_inputs(seed: int = 0) -> dict:
    key = jax.random.key(seed)
    ks = jax.random.split(key, 13)
    f32 = jnp.float32
    return {
        "x": jax.random.normal(ks[0], (BATCH, SEQ, D_MODEL), f32),
        "meta_tokens": jax.random.normal(ks[1], (N_META, D_MODEL), f32),
        "pre_norm_g": 1.0 + 0.02 * jax.random.normal(ks[2], (DEPTH, D_MODEL), f32),
        "w_in": jax.random.normal(ks[3], (DEPTH, D_MODEL, IN_PROJ_WIDTH), f32) * D_MODEL ** -0.5,
        "q_norm_g": 1.0 + 0.02 * jax.random.normal(ks[4], (DEPTH, HEAD_DIM), f32),
        "k_norm_g": 1.0 + 0.02 * jax.random.normal(ks[5], (DEPTH, HEAD_DIM), f32),
        "lambda_q1": 0.1 * jax.random.normal(ks[6], (DEPTH, B_QK_DIM), f32),
        "lambda_k1": 0.1 * jax.random.normal(ks[7], (DEPTH, B_QK_DIM), f32),
        "lambda_q2": 0.1 * jax.random.normal(ks[8], (DEPTH, B_QK_DIM), f32),
        "lambda_k2": 0.1 * jax.random.normal(ks[9], (DEPTH, B_QK_DIM), f32),
        "subln_g": 1.0 + 0.02 * jax.random.normal(ks[10], (DEPTH, B_V_DIM), f32),
        "w_out": jax.random.normal(ks[11], (DEPTH, MIX_WIDTH, D_MODEL), f32) * MIX_WIDTH ** -0.5,
        "post_norm_g": 1.0 + 0.02 * jax.random.normal(ks[12], (DEPTH, D_MODEL), f32),
    }


def reference(x, meta_tokens, pre_norm_g, w_in, q_norm_g, k_norm_g, lambda_q1, lambda_k1,
              lambda_q2, lambda_k2, subln_g, w_out, post_norm_g):
    b, n_real, _ = x.shape
    meta = jnp.broadcast_to(meta_tokens.astype(x.dtype)[None], (b, N_META, D_MODEL))
    h = jnp.concatenate([meta, x], axis=1)
    seq_len = N_META + n_real
    pos = jnp.arange(seq_len, dtype=jnp.int32)

    rows = n_real // GRID_W
    grid_row = jnp.concatenate([jnp.zeros((N_META,), jnp.int32),
                                jnp.repeat(jnp.arange(rows, dtype=jnp.int32), GRID_W)])
    grid_col = jnp.concatenate([jnp.zeros((N_META,), jnp.int32),
                                jnp.tile(jnp.arange(GRID_W, dtype=jnp.int32), rows)])
    ang_row = axial_angles(grid_row)
    ang_col = axial_angles(grid_col)

    for layer in range(DEPTH):
        lambda_init = 0.8 - 0.6 * math.exp(-0.3 * layer)
        hn = rmsnorm(h, pre_norm_g[layer])
        proj = jnp.einsum('bld,dc->blc', hn, w_in[layer])
        qa, ka, va, ga, qb, kb, vb, gb = jnp.split(proj, SPLIT_POINTS, axis=-1)
        ya = gqa_axial_mixer(qa, ka, va, q_norm_g[layer], k_norm_g[layer],
                             ang_row, ang_col, pos) * jax.nn.silu(ga)
        yb = diff_attn_mixer(qb, kb, vb, lambda_q1[layer], lambda_k1[layer], lambda_q2[layer],
                             lambda_k2[layer], subln_g[layer], lambda_init, pos) * jax.nn.silu(gb)
        y = jnp.einsum('blc,cd->bld', jnp.concatenate([ya, yb], axis=-1), w_out[layer])
        h = h + rmsnorm(y, post_norm_g[layer])

    return h[:, N_META:, :]
```

```python
import functools
import math

import jax
import jax.numpy as jnp
import numpy as np
from jax import lax
from jax.experimental import pallas as pl
from jax.experimental.pallas import tpu as pltpu

D_MODEL = 2048
N_META = 16
GRID_W = 64
HEAD_DIM = 128
A_HEADS = 8
A_KV_HEADS = 2
A_GROUP = A_HEADS // A_KV_HEADS
B_HEADS = 8
B_QK_DIM = HEAD_DIM // 2
ROPE_THETA = 10000.0
NORM_EPS = 1e-6
LAMBDA_INIT = 0.8 - 0.6 * math.exp(-0.3 * 0)

G_QA, G_KA, G_VA, G_GA = 0, 8, 10, 12
G_QB, G_KB, G_VB, G_GB = 20, 28, 36, 44
N_GROUPS = 52
IN_PROJ_WIDTH = N_GROUPS * HEAD_DIM

PROJ_TN = 256
VMEM_LIMIT = 56 * 1024 * 1024

_BF16 = jnp.bfloat16
_F32 = jnp.float32


def _rms_scale(a):
    return lax.rsqrt(jnp.mean(a * a, axis=-1, keepdims=True) + NORM_EPS)


def _in_proj_kernel(x_ref, g_ref, w_ref, cos_ref, sin_ref, qg_ref, kg_ref, o_ref, hn_ref):
    j = pl.program_id(1)

    @pl.when(j == 0)
    def _():
        x = x_ref[...]
        hn_ref[...] = (x * _rms_scale(x) * g_ref[...]).astype(_BF16)

    acc = jnp.dot(hn_ref[...], w_ref[...], preferred_element_type=_F32)
    halves = (acc[:, :HEAD_DIM], acc[:, HEAD_DIM:])

    def norm_rope(a, gain):
        a = a * _rms_scale(a) * gain
        lane = lax.broadcasted_iota(jnp.int32, (1, HEAD_DIM), 1)
        first_half = (lane % (HEAD_DIM // 2)) < (HEAD_DIM // 4)
        from_hi = pltpu.roll(a, HEAD_DIM - HEAD_DIM // 4, 1)
        from_lo = pltpu.roll(a, HEAD_DIM // 4, 1)
        partner = jnp.where(first_half, from_hi, from_lo)
        return a * cos_ref[...] + partner * sin_ref[...]

    def store(fn):
        for t in range(2):
            o_ref[t] = fn(halves[t]).astype(o_ref.dtype)

    is_qa = j < (G_KA // 2)
    is_ka = j == (G_KA // 2)
    is_gate = ((j >= G_GA // 2) & (j < G_QB // 2)) | (j >= G_GB // 2)
    is_qb = (j >= G_QB // 2) & (j < G_KB // 2)
    is_plain = jnp.logical_not(is_qa | is_ka | is_gate | is_qb)

    @pl.when(is_qa)
    def _():
        gain = qg_ref[...] * (1.0 / math.sqrt(HEAD_DIM))
        store(lambda a: norm_rope(a, gain))

    @pl.when(is_ka)
    def _():
        store(lambda a: norm_rope(a, kg_ref[...]))

    @pl.when(is_gate)
    def _():
        store(lambda a: a / (1.0 + jnp.exp(-a)))

    @pl.when(is_qb)
    def _():
        store(lambda a: a * (1.0 / math.sqrt(B_QK_DIM)))

    @pl.when(is_plain)
    def _():
        store(lambda a: a)


def _in_proj(x2d, pre_g, w_bf16, cos, sin, qg, kg, tm):
    m = x2d.shape[0]
    n_pos_blocks = cos.shape[0] // tm
    grid = (m // tm, IN_PROJ_WIDTH // PROJ_TN)
    return pl.pallas_call(
        _in_proj_kernel,
        grid=grid,
        in_specs=[
            pl.BlockSpec((tm, D_MODEL), lambda i, j: (i, 0)),
            pl.BlockSpec((1, D_MODEL), lambda i, j: (0, 0)),
            pl.BlockSpec((D_MODEL, PROJ_TN), lambda i, j: (0, j)),
            pl.BlockSpec((tm, HEAD_DIM), lambda i, j: (i % n_pos_blocks, 0)),
            pl.BlockSpec((tm, HEAD_DIM), lambda i, j: (i % n_pos_blocks, 0)),
            pl.BlockSpec((1, HEAD_DIM), lambda i, j: (0, 0)),
            pl.BlockSpec((1, HEAD_DIM), lambda i, j: (0, 0)),
        ],
        out_specs=pl.BlockSpec((2, tm, HEAD_DIM), lambda i, j: (j, i, 0)),
        out_shape=jax.ShapeDtypeStruct((N_GROUPS, m, HEAD_DIM), _BF16),
        scratch_shapes=[pltpu.VMEM((tm, D_MODEL), _BF16)],
        compiler_params=pltpu.CompilerParams(
            dimension_semantics=("parallel", "arbitrary"),
            vmem_limit_bytes=VMEM_LIMIT),
        name="in_proj",
    )(x2d, pre_g, w_bf16, cos, sin, qg, kg)


_NT = (((1,), (1,)), ((), ()))


def _attn_a_kernel(q_ref, k_ref, v_ref, km_ref, vm_ref, gate_ref, o_ref, *, tq):
    q = q_ref[...].reshape(A_GROUP * tq, HEAD_DIM)
    s = lax.dot_general(q, k_ref[0], _NT, preferred_element_type=_F32)
    sm = lax.dot_general(q, km_ref[0], _NT, preferred_element_type=_F32)
    m = jnp.maximum(jnp.max(s, axis=-1, keepdims=True), jnp.max(sm, axis=-1, keepdims=True))
    p = jnp.exp(s - m)
    pm = jnp.exp(sm - m)
    l = jnp.sum(p, axis=-1, keepdims=True) + jnp.sum(pm, axis=-1, keepdims=True)
    o = jnp.dot(p.astype(_BF16), v_ref[0], preferred_element_type=_F32)
    o = o + jnp.dot(pm.astype(_BF16), vm_ref[0], preferred_element_type=_F32)
    o = o / l
    for g in range(A_GROUP):
        og = o[g * tq:(g + 1) * tq] * gate_ref[g].astype(_F32)
        o_ref[:, g * HEAD_DIM:(g + 1) * HEAD_DIM] = og.astype(o_ref.dtype)


def _attn_a(proj, proj_meta, batch, seq, tq):
    nq = seq // tq
    rows = batch * seq
    return pl.pallas_call(
        functools.partial(_attn_a_kernel, tq=tq),
        grid=(batch, A_KV_HEADS, nq),
        in_specs=[
            pl.BlockSpec((A_GROUP, tq, HEAD_DIM), lambda b, h, i: (G_QA // A_GROUP + h, b * nq + i, 0)),
            pl.BlockSpec((1, seq, HEAD_DIM), lambda b, h, i: (G_KA + h, b, 0)),
            pl.BlockSpec((1, seq, HEAD_DIM), lambda b, h, i: (G_VA + h, b, 0)),
            pl.BlockSpec((1, N_META, HEAD_DIM), lambda b, h, i: (G_KA + h, 0, 0)),
            pl.BlockSpec((1, N_META, HEAD_DIM), lambda b, h, i: (G_VA + h, 0, 0)),
            pl.BlockSpec((A_GROUP, tq, HEAD_DIM), lambda b, h, i: (G_GA // A_GROUP + h, b * nq + i, 0)),
        ],
        out_specs=pl.BlockSpec((tq, A_GROUP * HEAD_DIM), lambda b, h, i: (b * nq + i, h)),
        out_shape=jax.ShapeDtypeStruct((rows, A_HEADS * HEAD_DIM), _BF16),
        compiler_params=pltpu.CompilerParams(
            dimension_semantics=("parallel", "parallel", "arbitrary"),
            vmem_limit_bytes=VMEM_LIMIT),
        name="attn_a",
    )(proj, proj, proj, proj_meta, proj_meta, proj)


def _attn_b_kernel(q_ref, k_ref, v_ref, km_ref, vm_ref, gate_ref,
                   lq1_ref, lk1_ref, lq2_ref, lk2_ref, subg_ref, o_ref, *, tq, seq):
    h = pl.program_id(1)
    i = pl.program_id(2)
    q = q_ref[0]
    lane = lax.broadcasted_iota(jnp.int32, (1, HEAD_DIM), 1)
    zero = jnp.zeros_like(q)
    qq = jnp.concatenate([jnp.where(lane < B_QK_DIM, q, zero),
                          jnp.where(lane >= B_QK_DIM, q, zero)], axis=0)
    s = lax.dot_general(qq, k_ref[0], _NT, preferred_element_type=_F32)
    sm = lax.dot_general(qq, km_ref[0], _NT, preferred_element_type=_F32)

    slope = jnp.exp2(-(jnp.zeros((1, 1), _F32) + (h + 1).astype(_F32)))
    qpos = i * tq + lax.broadcasted_iota(jnp.int32, (tq, 1), 0)
    kpos = lax.broadcasted_iota(jnp.int32, (1, seq), 1)
    bias = -slope * jnp.abs(qpos - kpos).astype(_F32)
    mpos = lax.broadcasted_iota(jnp.int32, (1, N_META), 1)
    bias_m = -slope * (qpos + N_META - mpos).astype(_F32)

    lam = (jnp.exp(jnp.sum(lq1_ref[...] * lk1_ref[...], axis=-1, keepdims=True))
           - jnp.exp(jnp.sum(lq2_ref[...] * lk2_ref[...], axis=-1, keepdims=True))
           + LAMBDA_INIT)

    ps, pms, ls = [], [], []
    for t in range(2):
        st = s[t * tq:(t + 1) * tq] + bias
        smt = sm[t * tq:(t + 1) * tq] + bias_m
        m = jnp.maximum(jnp.max(st, axis=-1, keepdims=True), jnp.max(smt, axis=-1, keepdims=True))
        p = jnp.exp(st - m)
        pm = jnp.exp(smt - m)
        ls.append(jnp.sum(p, axis=-1, keepdims=True) + jnp.sum(pm, axis=-1, keepdims=True))
        ps.append(p.astype(_BF16))
        pms.append(pm.astype(_BF16))
    pp = jnp.concatenate(ps, axis=0)
    ppm = jnp.concatenate(pms, axis=0)
    o = jnp.dot(pp, v_ref[0], preferred_element_type=_F32)
    o = o + jnp.dot(ppm, vm_ref[0], preferred_element_type=_F32)
    o = o[:tq] / ls[0] - lam * (o[tq:] / ls[1])
    o = o * _rms_scale(o) * subg_ref[...] * (1.0 - LAMBDA_INIT)
    o_ref[...] = (o * gate_ref[0].astype(_F32)).astype(o_ref.dtype)


def _attn_b(proj, proj_meta, lq1, lk1, lq2, lk2, subg, batch, seq, tq):
    nq = seq // tq
    rows = batch * seq
    vec = lambda n: pl.BlockSpec((1, n), lambda b, h, i: (0, 0))
    return pl.pallas_call(
        functools.partial(_attn_b_kernel, tq=tq, seq=seq),
        grid=(batch, B_HEADS, nq),
        in_specs=[
            pl.BlockSpec((1, tq, HEAD_DIM), lambda b, h, i: (G_QB + h, b * nq + i, 0)),
            pl.BlockSpec((1, seq, HEAD_DIM), lambda b, h, i: (G_KB + h, b, 0)),
            pl.BlockSpec((1, seq, HEAD_DIM), lambda b, h, i: (G_VB + h, b, 0)),
            pl.BlockSpec((1, N_META, HEAD_DIM), lambda b, h, i: (G_KB + h, 0, 0)),
            pl.BlockSpec((1, N_META, HEAD_DIM), lambda b, h, i: (G_VB + h, 0, 0)),
            pl.BlockSpec((1, tq, HEAD_DIM), lambda b, h, i: (G_GB + h, b * nq + i, 0)),
            vec(B_QK_DIM), vec(B_QK_DIM), vec(B_QK_DIM), vec(B_QK_DIM), vec(HEAD_DIM),
        ],
        out_specs=pl.BlockSpec((tq, HEAD_DIM), lambda b, h, i: (b * nq + i, h)),
        out_shape=jax.ShapeDtypeStruct((rows, B_HEADS * HEAD_DIM), _BF16),
        compiler_params=pltpu.CompilerParams(
            dimension_semantics=("parallel", "parallel", "arbitrary"),
            vmem_limit_bytes=VMEM_LIMIT),
        name="attn_b",
    )(proj, proj, proj, proj_meta, proj_meta, proj, lq1, lk1, lq2, lk2, subg)


def _out_proj_kernel(ya_ref, yb_ref, wa_ref, wb_ref, g_ref, x_ref, o_ref):
    y = jnp.dot(ya_ref[...], wa_ref[...], preferred_element_type=_F32)
    y = y + jnp.dot(yb_ref[...], wb_ref[...], preferred_element_type=_F32)
    o_ref[...] = x_ref[...] + y * _rms_scale(y) * g_ref[...]


def _out_proj(ya, yb, w_bf16, post_g, x2d, tm):
    m = x2d.shape[0]
    half = D_MODEL // 2
    return pl.pallas_call(
        _out_proj_kernel,
        grid=(m // tm,),
        in_specs=[
            pl.BlockSpec((tm, half), lambda i: (i, 0)),
            pl.BlockSpec((tm, half), lambda i: (i, 0)),
            pl.BlockSpec((half, D_MODEL), lambda i: (0, 0)),
            pl.BlockSpec((half, D_MODEL), lambda i: (1, 0)),
            pl.BlockSpec((1, D_MODEL), lambda i: (0, 0)),
            pl.BlockSpec((tm, D_MODEL), lambda i: (i, 0)),
        ],
        out_specs=pl.BlockSpec((tm, D_MODEL), lambda i: (i, 0)),
        out_shape=jax.ShapeDtypeStruct((m, D_MODEL), _F32),
        compiler_params=pltpu.CompilerParams(
            dimension_semantics=("parallel",),
            vmem_limit_bytes=VMEM_LIMIT),
        name="out_proj",
    )(ya, yb, w_bf16, w_bf16, post_g, x2d)


def _rope_tables(seq):
    pos = jnp.arange(seq, dtype=jnp.int32)
    axis_dim = HEAD_DIM // 2
    inv_freq = ROPE_THETA ** (-jnp.arange(0, axis_dim, 2, dtype=_F32) / axis_dim)
    ang_r = (pos // GRID_W).astype(_F32)[:, None] * inv_freq[None, :]
    ang_c = (pos % GRID_W).astype(_F32)[:, None] * inv_freq[None, :]
    cr, sr, cc, sc = jnp.cos(ang_r), jnp.sin(ang_r), jnp.cos(ang_c), jnp.sin(ang_c)
    cos = jnp.concatenate([cr, cr, cc, cc], axis=-1)
    sin = jnp.concatenate([-sr, sr, -sc, sc], axis=-1)
    return cos, sin


def kernel(x, meta_tokens, pre_norm_g, w_in, q_norm_g, k_norm_g, lambda_q1, lambda_k1,
           lambda_q2, lambda_k2, subln_g, w_out, post_norm_g):
    batch, seq, _ = x.shape
    x2d = x.reshape(batch * seq, D_MODEL)
    w_in_b = w_in[0].astype(_BF16)
    w_out_b = w_out[0].astype(_BF16)
    cos, sin = _rope_tables(seq)
    cos_m = jnp.ones((N_META, HEAD_DIM), _F32)
    sin_m = jnp.zeros((N_META, HEAD_DIM), _F32)

    proj = _in_proj(x2d, pre_norm_g, w_in_b, cos, sin, q_norm_g, k_norm_g, tm=1024)
    proj_meta = _in_proj(meta_tokens, pre_norm_g, w_in_b, cos_m, sin_m, q_norm_g, k_norm_g,
                         tm=N_META)
    ya = _attn_a(proj, proj_meta, batch, seq, tq=128)
    yb = _attn_b(proj, proj_meta, lambda_q1, lambda_k1, lambda_q2, lambda_k2, subln_g,
                 batch, seq, tq=256)
    out = _out_proj(ya, yb, w_out_b, post_norm_g, x2d, tm=256)
    return out.reshape(batch, seq, D_MODEL)
```

```python
import functools
import math

import jax
import jax.numpy as jnp
from jax import lax
from jax.experimental import pallas as pl
from jax.experimental.pallas import tpu as pltpu

D_MODEL = 2048
N_META = 16
GRID_W = 64
HEAD_DIM = 128
A_HEADS = 8
A_KV_HEADS = 2
A_GROUP = A_HEADS // A_KV_HEADS
B_HEADS = 8
B_QK_DIM = HEAD_DIM // 2
ROPE_THETA = 10000.0
NORM_EPS = 1e-6
LAMBDA_INIT = 0.8 - 0.6 * math.exp(-0.3 * 0)
LOG2E = math.log2(math.e)

G_QA, G_KA, G_VA, G_GA = 0, 8, 10, 12
G_QB, G_KB, G_VB, G_GB = 20, 28, 36, 44
N_GROUPS = 52
IN_PROJ_WIDTH = N_GROUPS * HEAD_DIM

PROJ_TN = 256
LANES = 128
SCORE_COLS = 256
TILES_PER_STEP = 8
VMEM_LIMIT = 56 * 1024 * 1024

_BF16 = jnp.bfloat16
_F32 = jnp.float32
_NT = (((1,), (1,)), ((), ()))


def _rms_scale(a):
    return lax.rsqrt(jnp.mean(a * a, axis=-1, keepdims=True) + NORM_EPS)


def _in_proj_kernel(x_ref, g_ref, w_ref, cos_ref, sin_ref, qg_ref, kg_ref, o_ref, hn_ref):
    j = pl.program_id(1)

    @pl.when(j == 0)
    def _():
        x = x_ref[...]
        hn_ref[...] = (x * _rms_scale(x) * g_ref[...]).astype(_BF16)

    acc = jnp.dot(hn_ref[...], w_ref[...], preferred_element_type=_F32)
    halves = (acc[:, :HEAD_DIM], acc[:, HEAD_DIM:])

    def norm_rope(a, gain):
        a = a * _rms_scale(a) * gain
        lane = lax.broadcasted_iota(jnp.int32, (1, HEAD_DIM), 1)
        first_half = (lane % (HEAD_DIM // 2)) < (HEAD_DIM // 4)
        from_hi = pltpu.roll(a, HEAD_DIM - HEAD_DIM // 4, 1)
        from_lo = pltpu.roll(a, HEAD_DIM // 4, 1)
        partner = jnp.where(first_half, from_hi, from_lo)
        return a * cos_ref[...] + partner * sin_ref[...]

    def store(fn):
        for t in range(2):
            o_ref[t] = fn(halves[t]).astype(o_ref.dtype)

    is_qa = j < (G_KA // 2)
    is_ka = j == (G_KA // 2)
    is_gate = ((j >= G_GA // 2) & (j < G_QB // 2)) | (j >= G_GB // 2)
    is_qb = (j >= G_QB // 2) & (j < G_KB // 2)
    is_plain = jnp.logical_not(is_qa | is_ka | is_gate | is_qb)

    @pl.when(is_qa)
    def _():
        gain = qg_ref[...] * (LOG2E / math.sqrt(HEAD_DIM))
        store(lambda a: norm_rope(a, gain))

    @pl.when(is_ka)
    def _():
        store(lambda a: norm_rope(a, kg_ref[...]))

    @pl.when(is_gate)
    def _():
        store(lambda a: a / (1.0 + jnp.exp(-a)))

    @pl.when(is_qb)
    def _():
        store(lambda a: a * (LOG2E / math.sqrt(B_QK_DIM)))

    @pl.when(is_plain)
    def _():
        store(lambda a: a)


def _in_proj(x2d, pre_g, w_bf16, cos, sin, qg, kg, tm):
    m = x2d.shape[0]
    n_pos_blocks = cos.shape[0] // tm
    grid = (m // tm, IN_PROJ_WIDTH // PROJ_TN)
    return pl.pallas_call(
        _in_proj_kernel,
        grid=grid,
        in_specs=[
            pl.BlockSpec((tm, D_MODEL), lambda i, j: (i, 0)),
            pl.BlockSpec((1, D_MODEL), lambda i, j: (0, 0)),
            pl.BlockSpec((D_MODEL, PROJ_TN), lambda i, j: (0, j)),
            pl.BlockSpec((tm, HEAD_DIM), lambda i, j: (i % n_pos_blocks, 0)),
            pl.BlockSpec((tm, HEAD_DIM), lambda i, j: (i % n_pos_blocks, 0)),
            pl.BlockSpec((1, HEAD_DIM), lambda i, j: (0, 0)),
            pl.BlockSpec((1, HEAD_DIM), lambda i, j: (0, 0)),
        ],
        out_specs=pl.BlockSpec((2, tm, HEAD_DIM), lambda i, j: (j, i, 0)),
        out_shape=jax.ShapeDtypeStruct((N_GROUPS, m, HEAD_DIM), _BF16),
        scratch_shapes=[pltpu.VMEM((tm, D_MODEL), _BF16)],
        compiler_params=pltpu.CompilerParams(
            dimension_semantics=("parallel", "arbitrary"),
            vmem_limit_bytes=VMEM_LIMIT),
        name="in_proj",
    )(x2d, pre_g, w_bf16, cos, sin, qg, kg)


def _padded_keys(n_keys):
    return -(-n_keys // SCORE_COLS) * SCORE_COLS


def _assemble_keys(k_ref, v_ref, km_ref, vm_ref, kall, vall, vt, p_bufs):
    seq = k_ref.shape[1]
    n_keys = N_META + seq
    kall[0:N_META, :] = km_ref[0]
    kall[N_META:n_keys, :] = k_ref[0]
    vall[0:N_META, :] = vm_ref[0]
    vall[N_META:n_keys, :] = v_ref[0]
    vall[n_keys:, :] = jnp.zeros((vall.shape[0] - n_keys, HEAD_DIM), _BF16)
    row = lax.broadcasted_iota(jnp.int32, (HEAD_DIM, HEAD_DIM), 0)
    col = lax.broadcasted_iota(jnp.int32, (HEAD_DIM, HEAD_DIM), 1)
    eye = (row == col).astype(_BF16)
    vt[...] = lax.dot_general(eye, vall[...], _NT, preferred_element_type=_F32).astype(_BF16)
    for p in p_bufs:
        p[n_keys:, :] = jnp.zeros((p.shape[0] - n_keys, SCORE_COLS), _BF16)


def _pipeline(n_tiles, stage_s, stage_e, stage_o):
    carry_s = {0: stage_s(0)}
    carry_e = {}
    for j in range(n_tiles):
        if j + 1 < n_tiles:
            carry_s[j + 1] = stage_s(j + 1)
        carry_e[j] = stage_e(j, carry_s.pop(j))
        if j >= 1:
            stage_o(j - 1, carry_e.pop(j - 1))
    stage_o(n_tiles - 1, carry_e.pop(n_tiles - 1))


def _attn_scratch(n_keys):
    kp = _padded_keys(n_keys)
    return [
        pltpu.VMEM((n_keys, HEAD_DIM), _BF16),
        pltpu.VMEM((kp, HEAD_DIM), _BF16),
        pltpu.VMEM((HEAD_DIM, kp), _BF16),
        pltpu.VMEM((n_keys, SCORE_COLS), _F32),
        pltpu.VMEM((n_keys, SCORE_COLS), _F32),
        pltpu.VMEM((kp, SCORE_COLS), _BF16),
        pltpu.VMEM((kp, SCORE_COLS), _BF16),
    ]


def _attn_a_kernel(q_ref, k_ref, v_ref, km_ref, vm_ref, gate_ref, o_ref,
                   kall, vall, vt, s0, s1, p0, p1):
    n_keys = kall.shape[0]
    tq = SCORE_COLS
    s_bufs, p_bufs = (s0, s1), (p0, p1)

    @pl.when(pl.program_id(2) == 0)
    def _():
        _assemble_keys(k_ref, v_ref, km_ref, vm_ref, kall, vall, vt, p_bufs)

    def stage_s(j):
        st = lax.dot_general(kall[...], q_ref[0, j * tq:(j + 1) * tq, :], _NT,
                             preferred_element_type=_F32)
        s_bufs[j % 2][...] = st
        return jnp.max(st, axis=0, keepdims=True)

    def stage_e(j, m):
        p = jnp.exp2(s_bufs[j % 2][...] - m)
        p_bufs[j % 2][0:n_keys, :] = p.astype(_BF16)
        return jnp.sum(p, axis=0, keepdims=True)

    def stage_o(j, l):
        ot = jnp.dot(vt[...], p_bufs[j % 2][...], preferred_element_type=_F32)
        o = (ot * (1.0 / l)).T * gate_ref[0, j * tq:(j + 1) * tq, :].astype(_F32)
        o_ref[j * tq:(j + 1) * tq, :] = o.astype(o_ref.dtype)

    _pipeline(TILES_PER_STEP, stage_s, stage_e, stage_o)


def _attn_a(proj, proj_meta, batch, seq):
    rows = batch * seq
    assert seq == TILES_PER_STEP * SCORE_COLS
    return pl.pallas_call(
        _attn_a_kernel,
        grid=(batch, A_KV_HEADS, A_GROUP),
        in_specs=[
            pl.BlockSpec((1, seq, HEAD_DIM), lambda b, h, g: (G_QA + h * A_GROUP + g, b, 0)),
            pl.BlockSpec((1, seq, HEAD_DIM), lambda b, h, g: (G_KA + h, b, 0)),
            pl.BlockSpec((1, seq, HEAD_DIM), lambda b, h, g: (G_VA + h, b, 0)),
            pl.BlockSpec((1, N_META, HEAD_DIM), lambda b, h, g: (G_KA + h, 0, 0)),
            pl.BlockSpec((1, N_META, HEAD_DIM), lambda b, h, g: (G_VA + h, 0, 0)),
            pl.BlockSpec((1, seq, HEAD_DIM), lambda b, h, g: (G_GA + h * A_GROUP + g, b, 0)),
        ],
        out_specs=pl.BlockSpec((seq, HEAD_DIM), lambda b, h, g: (b, h * A_GROUP + g)),
        out_shape=jax.ShapeDtypeStruct((rows, A_HEADS * HEAD_DIM), _BF16),
        scratch_shapes=_attn_scratch(N_META + seq),
        compiler_params=pltpu.CompilerParams(
            dimension_semantics=("arbitrary", "arbitrary", "arbitrary"),
            vmem_limit_bytes=VMEM_LIMIT),
        name="attn_a",
    )(proj, proj, proj, proj_meta, proj_meta, proj)


def _attn_b_kernel(q_ref, k_ref, v_ref, km_ref, vm_ref, gate_ref,
                   lq1_ref, lk1_ref, lq2_ref, lk2_ref, subg_ref, o_ref,
                   kall, vall, vt, s0, s1, p0, p1, bias_tab, *, seq):
    h = pl.program_id(0)
    b = pl.program_id(1)
    part = pl.program_id(2)
    n_keys = kall.shape[0]
    tq = SCORE_COLS // 2
    s_bufs, p_bufs = (s0, s1), (p0, p1)
    tab_origin = seq - tq + N_META

    @pl.when((b == 0) & (part == 0))
    def _():
        slope = jnp.exp2(-(jnp.zeros((1, 1), _F32) + (h + 1).astype(_F32))) * LOG2E
        u = lax.broadcasted_iota(jnp.int32, bias_tab.shape, 0)
        c = lax.broadcasted_iota(jnp.int32, bias_tab.shape, 1)
        bias_tab[...] = -slope * jnp.abs(u - tab_origin - c).astype(_F32)

    @pl.when(part == 0)
    def _():
        _assemble_keys(k_ref, v_ref, km_ref, vm_ref, kall, vall, vt, p_bufs)

    lam = (jnp.exp(jnp.sum(lq1_ref[...] * lk1_ref[...], axis=-1, keepdims=True))
           - jnp.exp(jnp.sum(lq2_ref[...] * lk2_ref[...], axis=-1, keepdims=True))
           + LAMBDA_INIT)
    lane = lax.broadcasted_iota(jnp.int32, (1, HEAD_DIM), 1)

    def stage_s(j):
        q = q_ref[0, j * tq:(j + 1) * tq, :]
        zero = jnp.zeros_like(q)
        qq = jnp.concatenate([jnp.where(lane < B_QK_DIM, q, zero),
                              jnp.where(lane >= B_QK_DIM, q, zero)], axis=0)
        st = lax.dot_general(kall[...], qq, _NT, preferred_element_type=_F32)
        first_query = (part * TILES_PER_STEP + j) * tq
        start = pl.multiple_of(seq - tq - first_query, tq)
        bias = bias_tab[pl.ds(start, n_keys), :]
        ms = []
        for t in range(2):
            sb = st[:, t * tq:(t + 1) * tq] + bias
            s_bufs[j % 2][:, t * tq:(t + 1) * tq] = sb
            ms.append(jnp.max(sb, axis=0, keepdims=True))
        return ms

    def stage_e(j, ms):
        ls = []
        for t in range(2):
            p = jnp.exp2(s_bufs[j % 2][:, t * tq:(t + 1) * tq] - ms[t])
            p_bufs[j % 2][0:n_keys, t * tq:(t + 1) * tq] = p.astype(_BF16)
            ls.append(jnp.sum(p, axis=0, keepdims=True))
        return ls

    def stage_o(j, ls):
        ot = jnp.dot(vt[...], p_bufs[j % 2][...], preferred_element_type=_F32)
        d = ot[:, :tq] * (1.0 / ls[0]) - ot[:, tq:] * (lam / ls[1])
        o = d.T
        o = o * _rms_scale(o) * subg_ref[...] * (1.0 - LAMBDA_INIT)
        o = o * gate_ref[0, j * tq:(j + 1) * tq, :].astype(_F32)
        o_ref[j * tq:(j + 1) * tq, :] = o.astype(o_ref.dtype)

    _pipeline(TILES_PER_STEP, stage_s, stage_e, stage_o)


def _attn_b(proj, proj_meta, lq1, lk1, lq2, lk2, subg, batch, seq):
    rows = batch * seq
    tq = SCORE_COLS // 2
    step_rows = TILES_PER_STEP * tq
    parts = seq // step_rows
    n_keys = N_META + seq
    vec = lambda n: pl.BlockSpec((1, n), lambda h, b, p: (0, 0))
    tab_rows = 2 * seq - tq + N_META
    return pl.pallas_call(
        functools.partial(_attn_b_kernel, seq=seq),
        grid=(B_HEADS, batch, parts),
        in_specs=[
            pl.BlockSpec((1, step_rows, HEAD_DIM), lambda h, b, p: (G_QB + h, b * parts + p, 0)),
            pl.BlockSpec((1, seq, HEAD_DIM), lambda h, b, p: (G_KB + h, b, 0)),
            pl.BlockSpec((1, seq, HEAD_DIM), lambda h, b, p: (G_VB + h, b, 0)),
            pl.BlockSpec((1, N_META, HEAD_DIM), lambda h, b, p: (G_KB + h, 0, 0)),
            pl.BlockSpec((1, N_META, HEAD_DIM), lambda h, b, p: (G_VB + h, 0, 0)),
            pl.BlockSpec((1, step_rows, HEAD_DIM), lambda h, b, p: (G_GB + h, b * parts + p, 0)),
            vec(B_QK_DIM), vec(B_QK_DIM), vec(B_QK_DIM), vec(B_QK_DIM), vec(HEAD_DIM),
        ],
        out_specs=pl.BlockSpec((step_rows, HEAD_DIM), lambda h, b, p: (b * parts + p, h)),
        out_shape=jax.ShapeDtypeStruct((rows, B_HEADS * HEAD_DIM), _BF16),
        scratch_shapes=_attn_scratch(n_keys) + [pltpu.VMEM((tab_rows, tq), _F32)],
        compiler_params=pltpu.CompilerParams(
            dimension_semantics=("arbitrary", "arbitrary", "arbitrary"),
            vmem_limit_bytes=VMEM_LIMIT),
        name="attn_b",
    )(proj, proj, proj, proj_meta, proj_meta, proj, lq1, lk1, lq2, lk2, subg)


def _out_proj_kernel(ya_ref, yb_ref, wa_ref, wb_ref, g_ref, x_ref, o_ref):
    y = jnp.dot(ya_ref[...], wa_ref[...], preferred_element_type=_F32)
    y = y + jnp.dot(yb_ref[...], wb_ref[...], preferred_element_type=_F32)
    o_ref[...] = x_ref[...] + y * _rms_scale(y) * g_ref[...]


def _out_proj(ya, yb, w_bf16, post_g, x2d, tm):
    m = x2d.shape[0]
    half = D_MODEL // 2
    return pl.pallas_call(
        _out_proj_kernel,
        grid=(m // tm,),
        in_specs=[
            pl.BlockSpec((tm, half), lambda i: (i, 0)),
            pl.BlockSpec((tm, half), lambda i: (i, 0)),
            pl.BlockSpec((half, D_MODEL), lambda i: (0, 0)),
            pl.BlockSpec((half, D_MODEL), lambda i: (1, 0)),
            pl.BlockSpec((1, D_MODEL), lambda i: (0, 0)),
            pl.BlockSpec((tm, D_MODEL), lambda i: (i, 0)),
        ],
        out_specs=pl.BlockSpec((tm, D_MODEL), lambda i: (i, 0)),
        out_shape=jax.ShapeDtypeStruct((m, D_MODEL), _F32),
        compiler_params=pltpu.CompilerParams(
            dimension_semantics=("parallel",),
            vmem_limit_bytes=VMEM_LIMIT),
        name="out_proj",
    )(ya, yb, w_bf16, w_bf16, post_g, x2d)


def _rope_tables(seq):
    pos = jnp.arange(seq, dtype=jnp.int32)
    axis_dim = HEAD_DIM // 2
    inv_freq = ROPE_THETA ** (-jnp.arange(0, axis_dim, 2, dtype=_F32) / axis_dim)
    ang_r = (pos // GRID_W).astype(_F32)[:, None] * inv_freq[None, :]
    ang_c = (pos % GRID_W).astype(_F32)[:, None] * inv_freq[None, :]
    cr, sr, cc, sc = jnp.cos(ang_r), jnp.sin(ang_r), jnp.cos(ang_c), jnp.sin(ang_c)
    cos = jnp.concatenate([cr, cr, cc, cc], axis=-1)
    sin = jnp.concatenate([-sr, sr, -sc, sc], axis=-1)
    return cos, sin


def kernel(x, meta_tokens, pre_norm_g, w_in, q_norm_g, k_norm_g, lambda_q1, lambda_k1,
           lambda_q2, lambda_k2, subln_g, w_out, post_norm_g):
    batch, seq, _ = x.shape
    x2d = x.reshape(batch * seq, D_MODEL)
    w_in_b = w_in[0].astype(_BF16)
    w_out_b = w_out[0].astype(_BF16)
    cos, sin = _rope_tables(seq)
    cos_m = jnp.ones((N_META, HEAD_DIM), _F32)
    sin_m = jnp.zeros((N_META, HEAD_DIM), _F32)

    proj = _in_proj(x2d, pre_norm_g, w_in_b, cos, sin, q_norm_g, k_norm_g, tm=1024)
    proj_meta = _in_proj(meta_tokens, pre_norm_g, w_in_b, cos_m, sin_m, q_norm_g, k_norm_g,
                         tm=N_META)
    ya = _attn_a(proj, proj_meta, batch, seq)
    yb = _attn_b(proj, proj_meta, lambda_q1, lambda_k1, lambda_q2, lambda_k2, subln_g,
                 batch, seq)
    out = _out_proj(ya, yb, w_out_b, post_norm_g, x2d, tm=256)
    return out.reshape(batch, seq, D_MODEL)
```
